```python
import math
import jax
import jax.numpy as jnp
from jax import lax
import numpy as np

D_MODEL = 1024
BATCH = 8
SEQ = 2048
DEPTH = 4
DEC_BATCH = 128
DEC_SEQ = 1
PAST_LEN = 2048
PAGE_SIZE = 128

HEAD_DIM = 64
N_HEADS = D_MODEL // HEAD_DIM
N_MIXERS = 3
BLK = 128
DIL_GROUPS = ((128, 1), (512, 4), (2048, 16))
NSA_KV_HEADS = 4
NSA_HPG = N_HEADS // NSA_KV_HEADS
CMP_BLOCK = 32
SEL_BLOCK = 64
SEL_TOP = 8
SEL_FORCE = 1.0e4
NSA_WINDOW = 512
DIFF_HEADS = N_HEADS // 2
DIFF_VDIM = 2 * HEAD_DIM
D_FF = -(-8 * D_MODEL // (3 * 256)) * 256
REL_BUCKETS = 32
REL_MAX_DIST = 2048
LN_EPS = 1e-5
RMS_EPS = 1e-5
NEG_INF = -1e30
ALPHA = (2 * DEPTH) ** 0.25
BETA = (8 * DEPTH) ** -0.25
W_IN_A = len(DIL_GROUPS) * 3 * N_HEADS * HEAD_DIM
W_IN_B = N_HEADS * HEAD_DIM + 3 * 2 * NSA_KV_HEADS * HEAD_DIM + 3 * N_HEADS
W_IN_C = 4 * DIFF_HEADS * HEAD_DIM + DIFF_HEADS * DIFF_VDIM
IN_WIDTH = (W_IN_A, W_IN_B, W_IN_C)
OUT_WIDTH = (N_HEADS * HEAD_DIM, N_HEADS * HEAD_DIM, DIFF_HEADS * DIFF_VDIM)

kernel_name = 'hybrid_dilated_nsa_diff_decoder_step'


def rel_bucket(dist):
    n = jnp.maximum(dist, 0)
    exact = REL_BUCKETS // 2
    nf = jnp.maximum(n, 1).astype(jnp.float32)
    large = exact + (jnp.log(nf / exact) / math.log(REL_MAX_DIST / exact) * (REL_BUCKETS - exact)).astype(jnp.int32)
    return jnp.where(n < exact, n, jnp.minimum(large, REL_BUCKETS - 1))


def masked_softmax(s, valid):
    s = jnp.where(valid, s, NEG_INF)
    m = jnp.max(s, axis=-1, keepdims=True)
    e = jnp.exp(s - m)
    z = jnp.sum(e, axis=-1, keepdims=True)
    return e / z, (m + jnp.log(z))[..., 0]


def layer_norm(x, g, b):
    xf = x.astype(jnp.float32)
    mu = jnp.mean(xf, -1, keepdims=True)
    var = jnp.mean(jnp.square(xf - mu), -1, keepdims=True)
    return ((xf - mu) * lax.rsqrt(var + LN_EPS) * g + b).astype(x.dtype)


def ada_modulation(c, ada_w, ada_b):
    mod = (jax.nn.silu(c) @ ada_w + ada_b)[:, None, :]
    return jnp.split(mod, 6, axis=-1)


def swiglu(h, w_gu, w_down):
    gt, up = jnp.split(h @ w_gu, 2, axis=-1)
    return (jax.nn.silu(gt) * up) @ w_down


def banded_attention(q, k, v, span, step, rel_bias):
    n, ln, nh, dh = q.shape
    hk = k.shape[2]
    rep = nh // hk
    nb = -(-ln // BLK)
    lp = nb * BLK
    nprev = -(-span // BLK)
    kw = (nprev + 1) * BLK
    qb = jnp.pad(q, ((0, 0), (0, lp - ln), (0, 0), (0, 0))).reshape(n, nb, BLK, hk, rep, dh)
    qb = jnp.moveaxis(qb, 1, 0)
    pad = ((0, 0), (nprev * BLK, lp - ln), (0, 0), (0, 0))
    kp = jnp.pad(k, pad)
    vp = jnp.pad(v, pad)
    delta = nprev * BLK + jnp.arange(BLK)[:, None] - jnp.arange(kw)[None, :]
    band = (delta >= 0) & (delta <= span)
    bias = rel_bias[rel_bucket(delta * step)].astype(jnp.float32)
    bias = bias.reshape(BLK, kw, hk, rep).transpose(2, 3, 0, 1)
    scale = dh ** -0.5

    def block(args):
        b, qblk = args
        kb = lax.dynamic_slice_in_dim(kp, b * BLK, kw, axis=1)
        vb = lax.dynamic_slice_in_dim(vp, b * BLK, kw, axis=1)
        valid = band & (((b - nprev) * BLK + jnp.arange(kw)) >= 0)[None, :]
        s = jnp.einsum('nqgrd,nkgd->ngrqk', qblk, kb).astype(jnp.float32) * scale + bias
        p, lse = masked_softmax(s, valid)
        o = jnp.einsum('ngrqk,nkgd->nqgrd', p.astype(vb.dtype), vb)
        return o, lse

    o, lse = lax.map(block, (jnp.arange(nb), qb))
    o = jnp.moveaxis(o, 0, 1).reshape(n, lp, nh, dh)[:, :ln]
    lse = lse.transpose(1, 0, 4, 2, 3).reshape(n, lp, nh)[:, :ln]
    return o, lse


def combine_by_denominator(outs, lses):
    w = jax.nn.softmax(jnp.stack(lses, 0), axis=0)
    o = jnp.einsum('gnth,gnthd->nthd', w.astype(outs[0].dtype), jnp.stack(outs, 0))
    return o.reshape(o.shape[0], o.shape[1], -1)


def to_residues(t, dil):
    b, s, h, d = t.shape
    return t.reshape(b, s // dil, dil, h, d).transpose(0, 2, 1, 3, 4).reshape(b * dil, s // dil, h, d)


def from_residues(t, b, dil):
    n, l = t.shape[:2]
    rest = t.shape[2:]
    t = t.reshape((b, dil, l) + rest)
    t = jnp.swapaxes(t, 1, 2)
    return t.reshape((b, l * dil) + rest)


def dilated_prompt(h, w_in, w_o, rel_bias):
    b, s, _ = h.shape
    proj = (h @ w_in).reshape(b, s, len(DIL_GROUPS), 3, N_HEADS, HEAD_DIM)
    outs, lses, states = [], [], []
    for g, (win, dil) in enumerate(DIL_GROUPS):
        q, k, v = proj[:, :, g, 0], proj[:, :, g, 1], proj[:, :, g, 2]
        o, lse = banded_attention(to_residues(q, dil), to_residues(k, dil), to_residues(v, dil),
                                  win // dil, dil, rel_bias)
        outs.append(from_residues(o, b, dil))
        lses.append(from_residues(lse, b, dil))
        keep = min(win, s)
        states.append(jnp.stack([k[:, s - keep:], v[:, s - keep:]], axis=2))
    return combine_by_denominator(outs, lses) @ w_o, states


def dilated_sample(h, bufs, w_in, w_o, rel_bias):
    n, t, _ = h.shape
    proj = (h @ w_in).reshape(n, t, len(DIL_GROUPS), 3, N_HEADS, HEAD_DIM)
    qpos = PAST_LEN + jnp.arange(t)
    scale = HEAD_DIM ** -0.5
    outs, lses, states = [], [], []
    for g, (win, dil) in enumerate(DIL_GROUPS):
        q = proj[:, :, g, 0]
        new_kv = jnp.stack([proj[:, :, g, 1], proj[:, :, g, 2]], axis=2)
        buf = bufs[g]
        wb = buf.shape[1]
        j = jnp.arange(win // dil + 1)
        kpos = qpos[:, None] - j[None, :] * dil
        valid = kpos >= 0
        idx = kpos - (PAST_LEN - wb)
        from_buf = (idx < wb)[None, :, :, None, None, None]
        kvg = jnp.where(from_buf, buf[:, jnp.clip(idx, 0, wb - 1)],
                        new_kv[:, jnp.clip(idx - wb, 0, t - 1)])
        bias = rel_bias[rel_bucket(j * dil)].astype(jnp.float32)
        s = jnp.einsum('nthd,ntjhd->nhtj', q, kvg[:, :, :, 0]).astype(jnp.float32) * scale + bias.T[:, None, :]
        p, lse = masked_softmax(s, valid)
        outs.append(jnp.einsum('nhtj,ntjhd->nthd', p.astype(kvg.dtype), kvg[:, :, :, 1]))
        lses.append(lse.transpose(0, 2, 1))
        states.append(new_kv)
    return combine_by_denominator(outs, lses) @ w_o, states


def nsa_split(h, w_in):
    n, t, _ = h.shape
    proj = h @ w_in
    nq = N_HEADS * HEAD_DIM
    nkv = 3 * 2 * NSA_KV_HEADS * HEAD_DIM
    q = proj[..., :nq].reshape(n, t, NSA_KV_HEADS, NSA_HPG, HEAD_DIM)
    kv = proj[..., nq:nq + nkv].reshape(n, t, 3, 2, NSA_KV_HEADS, HEAD_DIM)
    gates = jax.nn.sigmoid(proj[..., nq + nkv:].astype(jnp.float32)).reshape(n, t, NSA_KV_HEADS, NSA_HPG, 3)
    return q, kv, gates.astype(h.dtype)


def nsa_blocks(kvf, cmp_pe, cmp_w):
    n, ln = kvf.shape[:2]
    nc = ln // CMP_BLOCK
    cb = kvf[:, :nc * CMP_BLOCK, :2].reshape(n, nc, CMP_BLOCK, 2, NSA_KV_HEADS, HEAD_DIM)
    cb = cb + cmp_pe.transpose(1, 0, 2)[None, None, :, :, None, :]
    comp = jnp.einsum('nclkgd,klde->kngce', cb, cmp_w)
    nsel = -(-ln // SEL_BLOCK)
    sb = jnp.pad(kvf[:, :, 2:], ((0, 0), (0, nsel * SEL_BLOCK - ln), (0, 0), (0, 0), (0, 0)))
    sb = sb.reshape(n, nsel, SEL_BLOCK, 2, NSA_KV_HEADS, HEAD_DIM).transpose(3, 0, 4, 1, 2, 5)
    return comp[0], comp[1], sb[0], sb[1]


def nsa_cmp_sel(q, qpos, kc, vc, ks, vs, rel_bias):
    n, t, g, r, dh = q.shape
    nc = kc.shape[2]
    nsel = ks.shape[2]
    scale = dh ** -0.5
    s = jnp.einsum('ntgrd,ngcd->ngrtc', q, kc).astype(jnp.float32) * scale
    cvalid = ((jnp.arange(nc) + 1) * CMP_BLOCK - 1)[None, :] <= qpos[:, None]
    p, _ = masked_softmax(s, cvalid)
    p = p * jnp.any(cvalid, -1)[:, None]
    o_cmp = jnp.einsum('ngrtc,ngcd->ntgrd', p.astype(vc.dtype), vc)
    per = SEL_BLOCK // CMP_BLOCK
    imp = jnp.sum(p, axis=2)
    imp = jnp.pad(imp, ((0, 0), (0, 0), (0, 0), (0, nsel * per - nc))).reshape(n, g, t, nsel, per).sum(-1)
    blk = jnp.arange(nsel)
    forced = (blk[None, :] == 0) | (blk[None, :] == (qpos // SEL_BLOCK)[:, None])
    avail = blk[None, :] * SEL_BLOCK <= qpos[:, None]
    imp = jnp.where(forced, SEL_FORCE, jnp.where(avail, imp, -1.0))
    n_top = min(SEL_TOP, nsel)
    _, idx = lax.top_k(imp, n_top)
    take = jax.vmap(jax.vmap(lambda blocks, i: blocks[i]))
    kg = take(ks, idx).reshape(n, g, t, n_top * SEL_BLOCK, dh)
    vg = take(vs, idx).reshape(n, g, t, n_top * SEL_BLOCK, dh)
    kpos = (idx[..., None] * SEL_BLOCK + jnp.arange(SEL_BLOCK)).reshape(n, g, t, n_top * SEL_BLOCK)
    dist = qpos[None, None, :, None] - kpos
    rb = rel_bias.reshape(REL_BUCKETS, g, r).transpose(1, 0, 2)
    bias = rb[jnp.arange(g)[None, :, None, None], rel_bucket(dist)].astype(jnp.float32)
    s2 = jnp.einsum('ntgrd,ngtkd->ngrtk', q, kg).astype(jnp.float32) * scale + bias.transpose(0, 1, 4, 2, 3)
    p2, _ = masked_softmax(s2, (dist >= 0)[:, :, None])
    o_sel = jnp.einsum('ngrtk,ngtkd->ntgrd', p2.astype(vg.dtype), vg)
    return o_cmp, o_sel


def nsa_merge(gates, o_cmp, o_sel, o_win, w_o):
    o = gates[..., 0, None] * o_cmp + gates[..., 1, None] * o_sel + gates[..., 2, None] * o_win
    return o.reshape(o.shape[0], o.shape[1], -1) @ w_o


def nsa_prompt(h, w_in, w_o, cmp_pe, cmp_w, rel_bias):
    b, s, _ = h.shape
    q, kv, gates = nsa_split(h, w_in)
    kvf = kv[:, :, :2].reshape(b, s, 4, NSA_KV_HEADS, HEAD_DIM)
    kc, vc, ks, vs = nsa_blocks(kvf, cmp_pe, cmp_w)
    nb = s // BLK
    qb = jnp.moveaxis(q.reshape(b, nb, BLK, NSA_KV_HEADS, NSA_HPG, HEAD_DIM), 1, 0)
    pb = jnp.arange(s).reshape(nb, BLK)
    o_cmp, o_sel = lax.map(lambda a: nsa_cmp_sel(a[0], a[1], kc, vc, ks, vs, rel_bias), (qb, pb))
    o_cmp = jnp.moveaxis(o_cmp, 0, 1).reshape(b, s, NSA_KV_HEADS, NSA_HPG, HEAD_DIM)
    o_sel = jnp.moveaxis(o_sel, 0, 1).reshape(b, s, NSA_KV_HEADS, NSA_HPG, HEAD_DIM)
    o_win, _ = banded_attention(q.reshape(b, s, N_HEADS, HEAD_DIM), kv[:, :, 2, 0], kv[:, :, 2, 1],
                                NSA_WINDOW, 1, rel_bias)
    o_win = o_win.reshape(b, s, NSA_KV_HEADS, NSA_HPG, HEAD_DIM)
    keep = min(NSA_WINDOW, s)
    return nsa_merge(gates, o_cmp, o_sel, o_win, w_o), [kvf, kv[:, s - keep:, 2]]


def nsa_sample(h, pool, page_table, win_buf, w_in, w_o, cmp_pe, cmp_w, rel_bias):
    n, t, _ = h.shape
    q, kv, gates = nsa_split(h, w_in)
    new_rows = kv[:, :, :2].reshape(n, t, 4, NSA_KV_HEADS, HEAD_DIM)
    past = pool[page_table].reshape(n, -1, 4, NSA_KV_HEADS, HEAD_DIM)
    kvf = jnp.concatenate([past, new_rows], axis=1)
    kc, vc, ks, vs = nsa_blocks(kvf, cmp_pe, cmp_w)
    qpos = PAST_LEN + jnp.arange(t)
    o_cmp, o_sel = nsa_cmp_sel(q, qpos, kc, vc, ks, vs, rel_bias)
    wb = win_buf.shape[1]
    k_w = jnp.concatenate([win_buf[:, :, 0], kv[:, :, 2, 0]], axis=1)
    v_w = jnp.concatenate([win_buf[:, :, 1], kv[:, :, 2, 1]], axis=1)
    dist = qpos[:, None] - (PAST_LEN - wb + jnp.arange(wb + t))[None, :]
    valid = (dist >= 0) & (dist <= NSA_WINDOW)
    bias = rel_bias[rel_bucket(dist)].astype(jnp.float32)
    bias = bias.reshape(t, wb + t, NSA_KV_HEADS, NSA_HPG).transpose(2, 3, 0, 1)
    s = jnp.einsum('ntgrd,nkgd->ngrtk', q, k_w).astype(jnp.float32) * HEAD_DIM ** -0.5 + bias
    p, _ = masked_softmax(s, valid)
    o_win = jnp.einsum('ngrtk,nkgd->ntgrd', p.astype(v_w.dtype), v_w)
    return nsa_merge(gates, o_cmp, o_sel, o_win, w_o), [new_rows, kv[:, :, 2]]


def diff_split(h, w_in):
    n, t, _ = h.shape
    proj = h @ w_in
    nqk = DIFF_HEADS * 2 * HEAD_DIM
    q = proj[..., :nqk].reshape(n, t, DIFF_HEADS, 2, HEAD_DIM)
    k = proj[..., nqk:2 * nqk].reshape(n, t, DIFF_HEADS, 2, HEAD_DIM)
    v = proj[..., 2 * nqk:].reshape(n, t, DIFF_HEADS, DIFF_VDIM)
    return q, k, v


def diff_lambda(lam, lam_init):
    lf = lam.astype(jnp.float32)
    return jnp.exp(jnp.sum(lf[0] * lf[1])) - jnp.exp(jnp.sum(lf[2] * lf[3])) + lam_init


def diff_core(q, qpos, segments, lam_val, subln_g, lam_init, rel_bias):
    t = q.shape[1]
    scale = HEAD_DIM ** -0.5
    scores, masks = [], []
    for k, _, kpos in segments:
        dist = qpos[:, None] - kpos[None, :]
        bias = rel_bias[rel_bucket(dist)].astype(jnp.float32)
        bias = bias.reshape(t, kpos.shape[0], DIFF_HEADS, 2).transpose(2, 3, 0, 1)
        scores.append(jnp.einsum('nthmd,nshmd->nhmts', q, k).astype(jnp.float32) * scale + bias)
        masks.append(dist >= 0)
    p, _ = masked_softmax(jnp.concatenate(scores, -1), jnp.concatenate(masks, -1))
    attn = p[:, :, 0] - lam_val * p[:, :, 1]
    vdt = segments[0][1].dtype
    outs, off = [], 0
    for _, v, kpos in segments:
        ln = kpos.shape[0]
        outs.append(jnp.einsum('nhts,nshe->nthe', attn[..., off:off + ln].astype(vdt), v))
        off += ln
    o = outs[0]
    for extra in outs[1:]:
        o = o + extra
    of = o.astype(jnp.float32)
    of = of * lax.rsqrt(jnp.mean(jnp.square(of), -1, keepdims=True) + RMS_EPS) * subln_g * (1.0 - lam_init)
    return of.astype(vdt)


def diff_prompt(h, w_in, w_o, lam, subln_g, lam_init, rel_bias):
    b, s, _ = h.shape
    q, k, v = diff_split(h, w_in)
    lam_val = diff_lambda(lam, lam_init)
    nb = s // BLK
    qb = jnp.moveaxis(q.reshape(b, nb, BLK, DIFF_HEADS, 2, HEAD_DIM), 1, 0)
    pb = jnp.arange(s).reshape(nb, BLK)
    segs = ((k, v, jnp.arange(s)),)
    o = lax.map(lambda a: diff_core(a[0], a[1], segs, lam_val, subln_g, lam_init, rel_bias), (qb, pb))
    o = jnp.moveaxis(o, 0, 1).reshape(b, s, -1)
    rows = jnp.stack([k.reshape(b, s, DIFF_HEADS, 2 * HEAD_DIM), v], axis=2)
    return o @ w_o, [rows]


def diff_sample(h, pool, page_table, w_in, w_o, lam, subln_g, lam_init, rel_bias):
    n, t, _ = h.shape
    q, k, v = diff_split(h, w_in)
    k_past = pool[page_table, :, 0].reshape(n, -1, DIFF_HEADS, 2, HEAD_DIM)
    v_past = pool[page_table, :, 1].reshape(n, -1, DIFF_HEADS, DIFF_VDIM)
    qpos = PAST_LEN + jnp.arange(t)
    segs = ((k_past, v_past, jnp.arange(k_past.shape[1])), (k, v, qpos))
    o = diff_core(q, qpos, segs, diff_lambda(lam, lam_init), subln_g, lam_init, rel_bias)
    rows = jnp.stack([k.reshape(n, t, DIFF_HEADS, 2 * HEAD_DIM), v], axis=2)
    return o.reshape(n, t, -1) @ w_o, [rows]


def setup_inputs(seed: int = 0) -> dict:
    key = jax.random.key(seed)
    keys = iter(jax.random.split(key, 128))

    def nrm(shape, scale=1.0):
        return jax.random.normal(next(keys), shape, jnp.float32) * scale

    n_pages = PAST_LEN // PAGE_SIZE
    n_used = DEC_BATCH * n_pages
    n_phys = (5 * n_used + 3) // 4
    page_table = jax.random.permutation(next(keys), n_phys)[:n_used].reshape(DEC_BATCH, n_pages).astype(jnp.int32)

    def win_buf(win, kv_heads):
        return nrm((DEC_BATCH, min(win, PAST_LEN), 2, kv_heads, HEAD_DIM))

    inp = {
        'x_prompt': nrm((BATCH, SEQ, D_MODEL)),
        'x_sample': nrm((DEC_BATCH, DEC_SEQ, D_MODEL)),
        'c_prompt': nrm((BATCH, D_MODEL)),
        'c_sample': nrm((DEC_BATCH, D_MODEL)),
        'page_table': page_table,
    }
    for g, (win, _) in enumerate(DIL_GROUPS):
        inp['state_l0_win%d' % (g + 1)] = win_buf(win, N_HEADS)
    inp['cache_l1_kv'] = nrm((n_phys, PAGE_SIZE, 4, NSA_KV_HEADS, HEAD_DIM))
    inp['state_l1_win'] = win_buf(NSA_WINDOW, NSA_KV_HEADS)
    inp['cache_l2_kv'] = nrm((n_phys, PAGE_SIZE, 2, DIFF_HEADS, 2 * HEAD_DIM))
    for g, (win, _) in enumerate(DIL_GROUPS):
        inp['state_l3_win%d' % (g + 1)] = win_buf(win, N_HEADS)
    inp['rel_bias'] = nrm((REL_BUCKETS, N_HEADS), 0.2)
    for i in range(DEPTH):
        kind = i % N_MIXERS
        w_in_cols, w_out_rows = IN_WIDTH[kind], OUT_WIDTH[kind]
        inp['l%d_w_in' % i] = nrm((D_MODEL, w_in_cols), D_MODEL ** -0.5)
        inp['l%d_w_o' % i] = nrm((w_out_rows, D_MODEL), BETA * w_out_rows ** -0.5)
        inp['l%d_ada_w' % i] = nrm((D_MODEL, 6 * D_MODEL), 0.5 * D_MODEL ** -0.5)
        inp['l%d_ada_b' % i] = nrm((6 * D_MODEL,), 0.02)
        inp['l%d_ln_g' % i] = 1.0 + nrm((2, D_MODEL), 0.02)
        inp['l%d_ln_b' % i] = nrm((2, D_MODEL), 0.02)
        inp['l%d_w_gu' % i] = nrm((D_MODEL, 2 * D_FF), D_MODEL ** -0.5)
        inp['l%d_w_down' % i] = nrm((D_FF, D_MODEL), BETA * D_FF ** -0.5)
        if kind == 1:
            inp['l%d_cmp_pe' % i] = nrm((2, CMP_BLOCK, HEAD_DIM), 0.1)
            inp['l%d_cmp_w' % i] = nrm((2, CMP_BLOCK, HEAD_DIM, HEAD_DIM), (CMP_BLOCK * HEAD_DIM) ** -0.5)
        if kind == 2:
            inp['l%d_lam' % i] = nrm((4, HEAD_DIM), 0.1)
            inp['l%d_subln_g' % i] = 1.0 + nrm((DIFF_VDIM,), 0.02)
    return inp


def reference(x_prompt, x_sample, c_prompt, c_sample, page_table,
              state_l0_win1, state_l0_win2, state_l0_win3, cache_l1_kv, state_l1_win, cache_l2_kv,
              state_l3_win1, state_l3_win2, state_l3_win3, rel_bias,
              l0_w_in, l0_w_o, l0_ada_w, l0_ada_b, l0_ln_g, l0_ln_b, l0_w_gu, l0_w_down,
              l1_w_in, l1_w_o, l1_ada_w, l1_ada_b, l1_ln_g, l1_ln_b, l1_w_gu, l1_w_down, l1_cmp_pe, l1_cmp_w,
              l2_w_in, l2_w_o, l2_ada_w, l2_ada_b, l2_ln_g, l2_ln_b, l2_w_gu, l2_w_down, l2_lam, l2_subln_g,
              l3_w_in, l3_w_o, l3_ada_w, l3_ada_b, l3_ln_g, l3_ln_b, l3_w_gu, l3_w_down):
    layers = (
        (l0_w_in, l0_w_o, l0_ada_w, l0_ada_b, l0_ln_g, l0_ln_b, l0_w_gu, l0_w_down),
        (l1_w_in, l1_w_o, l1_ada_w, l1_ada_b, l1_ln_g, l1_ln_b, l1_w_gu, l1_w_down),
        (l2_w_in, l2_w_o, l2_ada_w, l2_ada_b, l2_ln_g, l2_ln_b, l2_w_gu, l2_w_down),
        (l3_w_in, l3_w_o, l3_ada_w, l3_ada_b, l3_ln_g, l3_ln_b, l3_w_gu, l3_w_down),
    )
    caches = ((state_l0_win1, state_l0_win2, state_l0_win3), (cache_l1_kv, state_l1_win), (cache_l2_kv,),
              (state_l3_win1, state_l3_win2, state_l3_win3))
    extras = ((), (l1_cmp_pe, l1_cmp_w), (l2_lam, l2_subln_g), ())
    xp, xs = x_prompt, x_sample
    new_p, new_s = [], []
    for i in range(DEPTH):
        w_in, w_o, ada_w, ada_b, ln_g, ln_b, w_gu, w_down = layers[i]
        mp = ada_modulation(c_prompt, ada_w, ada_b)
        ms = ada_modulation(c_sample, ada_w, ada_b)
        hp = xp * (1.0 + mp[1]) + mp[0]
        hs = xs * (1.0 + ms[1]) + ms[0]
        kind = i % N_MIXERS
        if kind == 0:
            yp, sp = dilated_prompt(hp, w_in, w_o, rel_bias)
            ys, ss = dilated_sample(hs, caches[i], w_in, w_o, rel_bias)
        elif kind == 1:
            cmp_pe, cmp_w = extras[i]
            pool, win = caches[i]
            yp, sp = nsa_prompt(hp, w_in, w_o, cmp_pe, cmp_w, rel_bias)
            ys, ss = nsa_sample(hs, pool, page_table, win, w_in, w_o, cmp_pe, cmp_w, rel_bias)
        else:
            lam, subln_g = extras[i]
            lam_init = 0.8 - 0.6 * math.exp(-0.3 * i)
            yp, sp = diff_prompt(hp, w_in, w_o, lam, subln_g, lam_init, rel_bias)
            ys, ss = diff_sample(hs, caches[i][0], page_table, w_in, w_o, lam, subln_g, lam_init, rel_bias)
        new_p += sp
        new_s += ss
        xp = layer_norm(ALPHA * xp + mp[2] * yp, ln_g[0], ln_b[0])
        xs = layer_norm(ALPHA * xs + ms[2] * ys, ln_g[0], ln_b[0])
        xp = layer_norm(ALPHA * xp + mp[5] * swiglu(xp * (1.0 + mp[4]) + mp[3], w_gu, w_down), ln_g[1], ln_b[1])
        xs = layer_norm(ALPHA * xs + ms[5] * swiglu(xs * (1.0 + ms[4]) + ms[3], w_gu, w_down), ln_g[1], ln_b[1])
    (p_l0_w1, p_l0_w2, p_l0_w3, p_l1_kv, p_l1_win, p_l2_kv, p_l3_w1, p_l3_w2, p_l3_w3) = new_p
    (s_l0_w1, s_l0_w2, s_l0_w3, s_l1_kv, s_l1_win, s_l2_kv, s_l3_w1, s_l3_w2, s_l3_w3) = new_s
    return (xp, xs, p_l0_w1, p_l0_w2, p_l0_w3, p_l1_kv, p_l1_win, p_l2_kv, p_l3_w1, p_l3_w2, p_l3_w3,
            s_l0_w1, s_l0_w2, s_l0_w3, s_l1_kv, s_l1_win, s_l2_kv, s_l3_w1, s_l3_w2, s_l3_w3)
```

```python
import functools
import math

import numpy as np
import jax
import jax.numpy as jnp
from jax import lax
from jax.experimental import pallas as pl
from jax.experimental.pallas import tpu as pltpu

F32 = jnp.float32
BF16 = jnp.bfloat16

D_MODEL = 1024
BATCH = 8
SEQ = 2048
DEPTH = 4
DEC_BATCH = 128
PAST_LEN = 2048
PAGE_SIZE = 128
N_PAGES = PAST_LEN // PAGE_SIZE

HEAD_DIM = 64
N_HEADS = 16
BLK = 128
DIL_GROUPS = ((128, 1), (512, 4), (2048, 16))
NSA_KV_HEADS = 4
NSA_HPG = 4
CMP_BLOCK = 32
SEL_BLOCK = 64
SEL_TOP = 8
SEL_FORCE = 1.0e4
NSA_WINDOW = 512
DIFF_HEADS = 8
DIFF_VDIM = 128
D_FF = 2816
REL_BUCKETS = 32
REL_MAX_DIST = 2048
LN_EPS = 1e-5
RMS_EPS = 1e-5
NEG_INF = -1e30
ALPHA = (2 * DEPTH) ** 0.25
SCALE = HEAD_DIM ** -0.5
W_IN_B = 2608
W_IN_B_PAD = 2688
N_CMP = PAST_LEN // CMP_BLOCK
N_SEL = SEQ // SEL_BLOCK

LANES = 128
VMEM_LIMIT_MB = 56


def _cparams(sem, vmem_mb=VMEM_LIMIT_MB):
    return pltpu.CompilerParams(dimension_semantics=sem, vmem_limit_bytes=vmem_mb * 1024 * 1024)


def _sds(shape, dtype=F32):
    return jax.ShapeDtypeStruct(shape, dtype)


def _sigmoid(x):
    return 1.0 / (1.0 + jnp.exp(-x))


def _dot(a, b):
    return jnp.dot(a, b, preferred_element_type=F32)


def _dot_nt(a, b):
    return lax.dot_general(a, b, (((1,), (1,)), ((), ())), preferred_element_type=F32)


def _layer_norm(y, g, b):
    mu = jnp.mean(y, axis=-1, keepdims=True)
    d = y - mu
    var = jnp.mean(d * d, axis=-1, keepdims=True)
    return d * lax.rsqrt(var + LN_EPS) * g + b


def _rel_bucket_np(dist):
    n = np.maximum(dist, 0).astype(np.int32)
    exact = REL_BUCKETS // 2
    nf = np.maximum(n, 1).astype(np.float32)
    large = exact + (np.log(nf / np.float32(exact)) / np.float32(math.log(REL_MAX_DIST / exact))
                     * np.float32(REL_BUCKETS - exact)).astype(np.int32)
    return np.where(n < exact, n, np.minimum(large, REL_BUCKETS - 1)).astype(np.int32)


def _bias_tiles(rel_bias, n_off, step):
    i = np.arange(BLK)[:, None]
    j = np.arange(BLK)[None, :]
    off = np.arange(n_off)[:, None, None]
    idx = _rel_bucket_np((off * BLK + i - j) * step)
    return jnp.transpose(rel_bias[idx], (0, 3, 1, 2))


def _bias_rows(rel_bias, dist):
    return jnp.transpose(rel_bias[_rel_bucket_np(np.asarray(dist))], (1, 0))


def _ada_kernel(c_ref, w_ref, b_ref, o_ref):
    c = c_ref[...]
    h = (c * _sigmoid(c)).astype(BF16)
    o_ref[...] = _dot(h, w_ref[...].astype(BF16)) + b_ref[...]


def _ada(c_all, ada_w, ada_b):
    n = c_all.shape[0]
    tn = 1536
    return pl.pallas_call(
        _ada_kernel,
        grid=(6 * D_MODEL // tn,),
        in_specs=[pl.BlockSpec((n, D_MODEL), lambda j: (0, 0)),
                  pl.BlockSpec((D_MODEL, tn), lambda j: (0, j)),
                  pl.BlockSpec((1, tn), lambda j: (0, j))],
        out_specs=pl.BlockSpec((n, tn), lambda j: (0, j)),
        out_shape=_sds((n, 6 * D_MODEL)),
        compiler_params=_cparams(("arbitrary",)),
        name="ada_mod",
    )(c_all, ada_w, ada_b.reshape(1, -1))


def _mod_matmul_kernel(x_ref, sh_ref, sc_ref, w_ref, o_ref):
    h = x_ref[0] * (1.0 + sc_ref[0]) + sh_ref[0]
    o_ref[0] = _dot(h.astype(BF16), w_ref[...])


def _mod_spec(per_row, tm, chunk, grid_rank3):
    if grid_rank3:
        if per_row:
            return pl.BlockSpec((1, tm, D_MODEL), lambda j, b, i: (b, i, chunk))
        return pl.BlockSpec((1, 1, D_MODEL), lambda j, b, i: (b, 0, chunk))
    if per_row:
        return pl.BlockSpec((1, tm, D_MODEL), lambda b, i: (b, i, chunk))
    return pl.BlockSpec((1, 1, D_MODEL), lambda b, i: (b, 0, chunk))


def _mod_matmul(x, mod, w_bf16, tn, tm):
    b, s, _ = x.shape
    n = w_bf16.shape[1]
    per_row = mod.shape[1] != 1
    return pl.pallas_call(
        _mod_matmul_kernel,
        grid=(n // tn, b, s // tm),
        in_specs=[pl.BlockSpec((1, tm, D_MODEL), lambda j, bb, i: (bb, i, 0)),
                  _mod_spec(per_row, tm, 0, True),
                  _mod_spec(per_row, tm, 1, True),
                  pl.BlockSpec((D_MODEL, tn), lambda j, bb, i: (0, j))],
        out_specs=pl.BlockSpec((1, tm, tn), lambda j, bb, i: (bb, i, j)),
        out_shape=_sds((b, s, n)),
        compiler_params=_cparams(("arbitrary", "arbitrary", "arbitrary")),
        name="mod_matmul",
    )(x, mod, mod, w_bf16)


def _oproj_kernel(*refs, n_in, combine):
    ins = refs[:n_in]
    x_ref, gate_ref, w_ref, lng_ref, lnb_ref, o_ref = refs[n_in:]
    if combine:
        outs = [ins[2 * g][0] for g in range(n_in // 2)]
        lses = [ins[2 * g + 1][0] for g in range(n_in // 2)]
        m = lses[0]
        for l in lses[1:]:
            m = jnp.maximum(m, l)
        ws = [jnp.exp(l - m) for l in lses]
        z = ws[0]
        for w in ws[1:]:
            z = z + w
        o = ws[0] * outs[0]
        for w, og in zip(ws[1:], outs[1:]):
            o = o + w * og
        o = o / z
    else:
        o = ins[0][0]
        for r in ins[1:]:
            o = o + r[0]
    y = _dot(o.astype(BF16), w_ref[...])
    o_ref[0] = _layer_norm(ALPHA * x_ref[0] + gate_ref[0] * y, lng_ref[...], lnb_ref[...])


def _oproj(ins, x, mod, w_bf16, ln_g, ln_b, tm, combine=False):
    b, s, _ = x.shape
    per_row = mod.shape[1] != 1
    kdim = w_bf16.shape[0]
    kern = functools.partial(_oproj_kernel, n_in=len(ins), combine=combine)
    row_spec = pl.BlockSpec((1, tm, kdim), lambda bb, i: (bb, i, 0))
    return pl.pallas_call(
        kern,
        grid=(b, s // tm),
        in_specs=[row_spec] * len(ins) + [
            pl.BlockSpec((1, tm, D_MODEL), lambda bb, i: (bb, i, 0)),
            _mod_spec(per_row, tm, 2, False),
            pl.BlockSpec((kdim, D_MODEL), lambda bb, i: (0, 0)),
            pl.BlockSpec((1, D_MODEL), lambda bb, i: (0, 0)),
            pl.BlockSpec((1, D_MODEL), lambda bb, i: (0, 0))],
        out_specs=pl.BlockSpec((1, tm, D_MODEL), lambda bb, i: (bb, i, 0)),
        out_shape=_sds((b, s, D_MODEL)),
        compiler_params=_cparams(("arbitrary", "arbitrary")),
        name="oproj_ln",
    )(*ins, x, mod, w_bf16, ln_g[0:1], ln_b[0:1])


FF_CHUNK = 1408
N_FF_CHUNKS = D_FF // FF_CHUNK


def _swiglu_kernel(x_ref, sh_ref, sc_ref, gate_ref, wgu_ref, wd_ref, lng_ref, lnb_ref, o_ref):
    x = x_ref[0]
    h = (x * (1.0 + sc_ref[0]) + sh_ref[0]).astype(BF16)
    acc = jnp.zeros(x.shape, F32)
    for c in range(N_FF_CHUNKS):
        gt = _dot(h, wgu_ref[:, c * FF_CHUNK:(c + 1) * FF_CHUNK])
        up = _dot(h, wgu_ref[:, D_FF + c * FF_CHUNK:D_FF + (c + 1) * FF_CHUNK])
        act = (gt * _sigmoid(gt) * up).astype(BF16)
        acc = acc + _dot(act, wd_ref[c * FF_CHUNK:(c + 1) * FF_CHUNK, :])
    o_ref[0] = _layer_norm(ALPHA * x + gate_ref[0] * acc, lng_ref[...], lnb_ref[...])


def _swiglu(x, mod, wgu_bf16, wd_bf16, ln_g, ln_b, tm):
    b, s, _ = x.shape
    per_row = mod.shape[1] != 1
    return pl.pallas_call(
        _swiglu_kernel,
        grid=(b, s // tm),
        in_specs=[pl.BlockSpec((1, tm, D_MODEL), lambda bb, i: (bb, i, 0)),
                  _mod_spec(per_row, tm, 3, False),
                  _mod_spec(per_row, tm, 4, False),
                  _mod_spec(per_row, tm, 5, False),
                  pl.BlockSpec((D_MODEL, 2 * D_FF), lambda bb, i: (0, 0)),
                  pl.BlockSpec((D_FF, D_MODEL), lambda bb, i: (0, 0)),
                  pl.BlockSpec((1, D_MODEL), lambda bb, i: (0, 0)),
                  pl.BlockSpec((1, D_MODEL), lambda bb, i: (0, 0))],
        out_specs=pl.BlockSpec((1, tm, D_MODEL), lambda bb, i: (bb, i, 0)),
        out_shape=_sds((b, s, D_MODEL)),
        compiler_params=_cparams(("arbitrary", "arbitrary")),
        name="swiglu_ln",
    )(x, mod, mod, mod, wgu_bf16, wd_bf16, ln_g[1:2], ln_b[1:2])


def _pair_list(n_blocks, nprev):
    qb, kb = [], []
    for q in range(n_blocks):
        for k in range(max(0, q - nprev), q + 1):
            qb.append(q)
            kb.append(k)
    return jnp.asarray(qb, jnp.int32), jnp.asarray(kb, jnp.int32)


def _flash_init(m_sc, l_sc, acc_sc):
    m_sc[...] = jnp.full(m_sc.shape, NEG_INF, F32)
    l_sc[...] = jnp.zeros(l_sc.shape, F32)
    acc_sc[...] = jnp.zeros(acc_sc.shape, F32)


def _flash_update(h, s, v, m_sc, l_sc, acc_sc, acc_lo, acc_hi):
    m_prev = m_sc[h]
    m_new = jnp.maximum(m_prev, jnp.max(s, axis=-1, keepdims=True))
    alpha = jnp.exp(m_prev - m_new)
    p = jnp.exp(s - m_new)
    l_sc[h] = alpha * l_sc[h] + jnp.sum(p, axis=-1, keepdims=True)
    acc_sc[:, acc_lo:acc_hi] = alpha * acc_sc[:, acc_lo:acc_hi] + _dot(p.astype(BF16), v)
    m_sc[h] = m_new


def _dist_tile(off):
    row = lax.broadcasted_iota(jnp.int32, (BLK, BLK), 0)
    col = lax.broadcasted_iota(jnp.int32, (BLK, BLK), 1)
    return off * BLK + row - col


def _banded_kernel(qb_ref, kb_ref, *refs, kv_of, span, nprev, with_lse, gate_branch):
    if gate_branch is None:
        q_ref, k_ref, v_ref, bias_ref = refs[:4]
        rest = refs[4:]
        gates_ref = None
    else:
        q_ref, k_ref, v_ref, bias_ref, gates_ref = refs[:5]
        rest = refs[5:]
    if with_lse:
        o_ref, lse_ref, m_sc, l_sc, acc_sc = rest
    else:
        o_ref, m_sc, l_sc, acc_sc = rest
        lse_ref = None
    t = pl.program_id(1)
    qb = qb_ref[t]
    kb = kb_ref[t]
    off = qb - kb

    @pl.when(kb == jnp.maximum(qb - nprev, 0))
    def _():
        _flash_init(m_sc, l_sc, acc_sc)

    dist = _dist_tile(off)
    valid = (dist >= 0) & (dist <= span)
    for h in range(N_HEADS):
        kh = kv_of[h]
        q = (q_ref[0, :, h * 64:(h + 1) * 64] * SCALE).astype(BF16)
        k = k_ref[0, :, kh * 64:(kh + 1) * 64].astype(BF16)
        v = v_ref[0, :, kh * 64:(kh + 1) * 64].astype(BF16)
        s = _dot_nt(q, k) + bias_ref[off, h]
        s = jnp.where(valid, s, NEG_INF)
        _flash_update(h, s, v, m_sc, l_sc, acc_sc, h * 64, (h + 1) * 64)

    @pl.when(kb == qb)
    def _():
        if gates_ref is not None:
            gates = _sigmoid(gates_ref[0])
        for h in range(N_HEADS):
            l = l_sc[h]
            o = acc_sc[:, h * 64:(h + 1) * 64] / l
            if gates_ref is not None:
                c = h * 3 + gate_branch
                o = o * gates[:, c:c + 1]
            o_ref[0, :, h * 64:(h + 1) * 64] = o
            if lse_ref is not None:
                lse_ref[0, :, h * 64:(h + 1) * 64] = jnp.broadcast_to(m_sc[h] + jnp.log(l), (BLK, 64))


def _banded_attention(q_arr, k_arr, v_arr, bias, *, n, n_blocks, kv_of, span, q_map, k_map, v_map, o_map,
                      out_arr_shape, with_lse, gates=None, gates_map=None, gate_branch=None):
    nprev = -(-span // BLK)
    qb, kb = _pair_list(n_blocks, nprev)
    hk = max(kv_of) + 1
    kern = functools.partial(_banded_kernel, kv_of=kv_of, span=span, nprev=nprev, with_lse=with_lse,
                             gate_branch=gate_branch)
    in_specs = [pl.BlockSpec((1, BLK, N_HEADS * 64), lambda i, t, qb_, kb_: q_map(i, qb_[t])),
                pl.BlockSpec((1, BLK, hk * 64), lambda i, t, qb_, kb_: k_map(i, kb_[t])),
                pl.BlockSpec((1, BLK, hk * 64), lambda i, t, qb_, kb_: v_map(i, kb_[t])),
                pl.BlockSpec((nprev + 1, N_HEADS, BLK, BLK), lambda i, t, qb_, kb_: (0, 0, 0, 0))]
    args = [q_arr, k_arr, v_arr, bias]
    if gates is not None:
        in_specs.append(pl.BlockSpec((1, BLK, LANES), lambda i, t, qb_, kb_: gates_map(i, qb_[t])))
        args.append(gates)
    o_spec = pl.BlockSpec((1, BLK, N_HEADS * 64), lambda i, t, qb_, kb_: o_map(i, qb_[t]))
    out_specs = [o_spec, o_spec] if with_lse else o_spec
    out_shape = [_sds(out_arr_shape), _sds(out_arr_shape)] if with_lse else _sds(out_arr_shape)
    return pl.pallas_call(
        kern,
        grid_spec=pltpu.PrefetchScalarGridSpec(
            num_scalar_prefetch=2,
            grid=(n, int(qb.shape[0])),
            in_specs=in_specs,
            out_specs=out_specs,
            scratch_shapes=[pltpu.VMEM((N_HEADS, BLK, 1), F32),
                            pltpu.VMEM((N_HEADS, BLK, 1), F32),
                            pltpu.VMEM((BLK, N_HEADS * 64), F32)]),
        out_shape=out_shape,
        compiler_params=_cparams(("arbitrary", "arbitrary")),
        name="banded_attention",
    )(qb, kb, *args)


def _dilated_prompt(x, mod, w_in_bf16, rel_bias):
    b, s, _ = x.shape
    proj = _mod_matmul(x, mod, w_in_bf16, tn=1024, tm=512)
    ins, states = [], []
    for g, (win, dil) in enumerate(DIL_GROUPS):
        ln = s // dil
        span = win // dil
        nprev = -(-span // BLK)
        bias = _bias_tiles(rel_bias, nprev + 1, dil)
        view = proj.reshape(b, ln, dil * 9216)
        base = g * 3

        def col_map(which, base=base, dil=dil):
            return lambda i, blk: (i // dil, blk, (i % dil) * 9 + base + which)

        def o_map(i, blk, dil=dil):
            return (i // dil, blk, i % dil)

        o, lse = _banded_attention(
            view, view, view, bias, n=b * dil, n_blocks=ln // BLK, kv_of=tuple(range(N_HEADS)), span=span,
            q_map=col_map(0), k_map=col_map(1), v_map=col_map(2), o_map=o_map,
            out_arr_shape=(b, ln, dil * 1024), with_lse=True)
        ins += [o.reshape(b, s, 1024), lse.reshape(b, s, 1024)]
        keep = min(win, s)
        kv = proj[:, s - keep:, g * 3072 + 1024:(g + 1) * 3072]
        states.append(kv.reshape(b, keep, 2, N_HEADS, HEAD_DIM))
    return ins, states


NSA_PERM = tuple((rho % 4) * 4 + rho // 4 for rho in range(N_HEADS))
NSA_KV_OF = tuple(rho % 4 for rho in range(N_HEADS))


def _nsa_weights(w_in, w_o, cmp_pe, cmp_w, rel_bias):
    nq = N_HEADS * HEAD_DIM
    nkv = 3 * 2 * NSA_KV_HEADS * HEAD_DIM
    wq = w_in[:, :nq].reshape(D_MODEL, 4, 4, 64).transpose(0, 2, 1, 3).reshape(D_MODEL, nq)
    wg = w_in[:, nq + nkv:].reshape(D_MODEL, 4, 4, 3).transpose(0, 2, 1, 3).reshape(D_MODEL, 48)
    pad = jnp.zeros((D_MODEL, W_IN_B_PAD - W_IN_B), w_in.dtype)
    w_in_p = jnp.concatenate([wq, w_in[:, nq:nq + nkv], wg, pad], axis=1).astype(BF16)
    w_o_p = w_o.reshape(4, 4, 64, D_MODEL).transpose(1, 0, 2, 3).reshape(nq, D_MODEL).astype(BF16)
    bd = jnp.einsum('gh,klde->klgdhe', jnp.eye(4, dtype=cmp_w.dtype), cmp_w).reshape(2, CMP_BLOCK, 256, 256)
    pe_t = jnp.tile(cmp_pe, (1, 1, 4)).reshape(2, CMP_BLOCK, 1, 256)
    rb_p = rel_bias[:, jnp.asarray(NSA_PERM)]
    return w_in_p, w_o_p, bd.astype(BF16), pe_t, rb_p


def _compress_rows(load, bd_ref, pe_ref, nb):
    accs = [jnp.zeros((nb * N_CMP, 256), F32) for _ in range(2)]
    for l in range(CMP_BLOCK):
        for kk in range(2):
            parts = []
            for i in range(nb):
                parts.append(load(i, l, kk))
                parts.append(load(i, CMP_BLOCK + l, kk))
            x = jnp.concatenate(parts, axis=0) + pe_ref[kk, l]
            accs[kk] = accs[kk] + _dot(x.astype(BF16), bd_ref[kk, l])
    return accs


def _compress_kernel(x0_ref, x1_ref, x2_ref, x3_ref, bd_ref, pe_ref, o_ref, *, nb):
    chunks = (x0_ref, x1_ref, x2_ref, x3_ref)

    def load(i, start, kk):
        rows = pl.ds(start, N_CMP // 2, stride=2 * CMP_BLOCK)
        return jnp.concatenate([chunks[2 * kk][i, rows, :], chunks[2 * kk + 1][i, rows, :]], axis=1)

    accs = _compress_rows(load, bd_ref, pe_ref, nb)
    for kk in range(2):
        for i in range(nb):
            o_ref[i, kk] = accs[kk][i * N_CMP:(i + 1) * N_CMP]


def _nsa_compress_prompt(proj, bd, pe_t):
    b, s, _ = proj.shape
    nb = 2
    return pl.pallas_call(
        functools.partial(_compress_kernel, nb=nb),
        grid=(b // nb,),
        in_specs=[pl.BlockSpec((nb, s, LANES), lambda i, c=c: (i, 0, 8 + c)) for c in range(4)] + [
                  pl.BlockSpec((2, CMP_BLOCK, 256, 256), lambda i: (0, 0, 0, 0)),
                  pl.BlockSpec((2, CMP_BLOCK, 1, 256), lambda i: (0, 0, 0, 0))],
        out_specs=pl.BlockSpec((nb, 2, N_CMP, 256), lambda i: (i, 0, 0, 0)),
        out_shape=_sds((b, 2, N_CMP, 256)),
        compiler_params=_cparams(("arbitrary",)),
        name="nsa_compress",
    )(proj, proj, proj, proj, bd, pe_t)


def _cmp_block_of_lane(shape, axis):
    lane = lax.broadcasted_iota(jnp.int32, shape, axis)
    return 2 * (lane % 32) + lane // 32


def _top_blocks(w, n_pick):
    blk = lax.broadcasted_iota(jnp.int32, w.shape, 1)
    sel = jnp.zeros(w.shape, F32)
    for _ in range(n_pick):
        mx = jnp.max(w, axis=-1, keepdims=True)
        idx = jnp.min(jnp.where(w == mx, blk, 1000), axis=-1, keepdims=True)
        pick = blk == idx
        sel = jnp.where(pick, 1.0, sel)
        w = jnp.where(pick, -2.0, w)
    return sel


def _nsa_cs_kernel(qb_ref, kb_ref, q_ref, ks_ref, vs_ref, kvc_ref, bias_ref, gates_ref, o_ref,
                   m_sc, l_sc, acc_sc, cmp_sc, sel_sc):
    t = pl.program_id(1)
    qb = qb_ref[t]
    kb = kb_ref[t]
    qpos = qb * BLK + lax.broadcasted_iota(jnp.int32, (BLK, 1), 0)

    @pl.when(kb == 0)
    def _():
        _flash_init(m_sc, l_sc, acc_sc)
        gates = _sigmoid(gates_ref[0])
        cvalid = ((_cmp_block_of_lane((BLK, N_CMP), 1) + 1) * CMP_BLOCK - 1) <= qpos
        anyv = qpos >= CMP_BLOCK - 1
        blk = lax.broadcasted_iota(jnp.int32, (BLK, N_SEL), 1)
        forced = (blk == 0) | (blk == qpos // SEL_BLOCK)
        avail = blk * SEL_BLOCK <= qpos
        for g in range(NSA_KV_HEADS):
            kc = kvc_ref[0, 0, :, g * 64:(g + 1) * 64].astype(BF16)
            vc = kvc_ref[0, 1, :, g * 64:(g + 1) * 64].astype(BF16)
            imp = jnp.zeros((BLK, N_CMP), F32)
            for r in range(NSA_HPG):
                rho = r * 4 + g
                q = (q_ref[0, :, rho * 64:(rho + 1) * 64] * SCALE).astype(BF16)
                s = jnp.where(cvalid, _dot_nt(q, kc), NEG_INF)
                e = jnp.exp(s - jnp.max(s, axis=-1, keepdims=True))
                p = e / jnp.sum(e, axis=-1, keepdims=True)
                p = jnp.where(anyv, p, 0.0)
                imp = imp + p
                c = rho * 3
                cmp_sc[:, rho * 64:(rho + 1) * 64] = gates[:, c:c + 1] * _dot(p.astype(BF16), vc)
            imp = imp[:, :N_SEL] + imp[:, N_SEL:]
            w = jnp.where(forced, SEL_FORCE, jnp.where(avail, imp, -1.0))
            sel_sc[g] = _top_blocks(w, SEL_TOP)

    off = qb - kb
    causal = _dist_tile(off) >= 0
    eb = lax.broadcasted_iota(jnp.int32, (N_SEL, BLK), 0)
    ek = lax.broadcasted_iota(jnp.int32, (N_SEL, BLK), 1)
    expand = jnp.where(eb == 2 * kb + ek // SEL_BLOCK, 1.0, 0.0).astype(BF16)
    for g in range(NSA_KV_HEADS):
        chosen = _dot(sel_sc[g].astype(BF16), expand) > 0.5
        valid = chosen & causal
        k = ks_ref[0, :, g * 64:(g + 1) * 64].astype(BF16)
        v = vs_ref[0, :, g * 64:(g + 1) * 64].astype(BF16)
        for r in range(NSA_HPG):
            rho = r * 4 + g
            q = (q_ref[0, :, rho * 64:(rho + 1) * 64] * SCALE).astype(BF16)
            s = jnp.where(valid, _dot_nt(q, k) + bias_ref[0, rho], NEG_INF)
            _flash_update(rho, s, v, m_sc, l_sc, acc_sc, rho * 64, (rho + 1) * 64)

    @pl.when(kb == qb)
    def _():
        gates = _sigmoid(gates_ref[0])
        for rho in range(N_HEADS):
            c = rho * 3 + 1
            o_sel = acc_sc[:, rho * 64:(rho + 1) * 64] / l_sc[rho]
            o_ref[0, :, rho * 64:(rho + 1) * 64] = cmp_sc[:, rho * 64:(rho + 1) * 64] + gates[:, c:c + 1] * o_sel


def _nsa_cmp_sel_prompt(proj, kvc, bias):
    b, s, _ = proj.shape
    nqb = s // BLK
    qb, kb = _pair_list(nqb, nqb)
    return pl.pallas_call(
        _nsa_cs_kernel,
        grid_spec=pltpu.PrefetchScalarGridSpec(
            num_scalar_prefetch=2,
            grid=(b, int(qb.shape[0])),
            in_specs=[pl.BlockSpec((1, BLK, 1024), lambda i, t, qb_, kb_: (i, qb_[t], 0)),
                      pl.BlockSpec((1, BLK, 256), lambda i, t, qb_, kb_: (i, kb_[t], 6)),
                      pl.BlockSpec((1, BLK, 256), lambda i, t, qb_, kb_: (i, kb_[t], 7)),
                      pl.BlockSpec((1, 2, N_CMP, 256), lambda i, t, qb_, kb_: (i, 0, 0, 0)),
                      pl.BlockSpec((1, N_HEADS, BLK, BLK), lambda i, t, qb_, kb_: (qb_[t] - kb_[t], 0, 0, 0)),
                      pl.BlockSpec((1, BLK, LANES), lambda i, t, qb_, kb_: (i, qb_[t], 20))],
            out_specs=pl.BlockSpec((1, BLK, 1024), lambda i, t, qb_, kb_: (i, qb_[t], 0)),
            scratch_shapes=[pltpu.VMEM((N_HEADS, BLK, 1), F32),
                            pltpu.VMEM((N_HEADS, BLK, 1), F32),
                            pltpu.VMEM((BLK, 1024), F32),
                            pltpu.VMEM((BLK, 1024), F32),
                            pltpu.VMEM((NSA_KV_HEADS, BLK, N_SEL), F32)]),
        out_shape=_sds((b, s, 1024)),
        compiler_params=_cparams(("arbitrary", "arbitrary")),
        name="nsa_cmp_sel",
    )(qb, kb, proj, proj, proj, kvc, bias, proj)


def _nsa_prompt(x, mod, w_in_p, bd, pe_t, rb_p):
    b, s, _ = x.shape
    proj = _mod_matmul(x, mod, w_in_p, tn=896, tm=512)
    kvc = _nsa_compress_prompt(proj, bd, pe_t)
    o_cs = _nsa_cmp_sel_prompt(proj, kvc, _bias_tiles(rb_p, s // BLK, 1))
    nprev = NSA_WINDOW // BLK
    o_win = _banded_attention(
        proj, proj, proj, _bias_tiles(rb_p, nprev + 1, 1), n=b, n_blocks=s // BLK, kv_of=NSA_KV_OF,
        span=NSA_WINDOW,
        q_map=lambda i, blk: (i, blk, 0), k_map=lambda i, blk: (i, blk, 8), v_map=lambda i, blk: (i, blk, 9),
        o_map=lambda i, blk: (i, blk, 0), out_arr_shape=(b, s, 1024), with_lse=False,
        gates=proj, gates_map=lambda i, blk: (i, blk, 20), gate_branch=2)
    kvf = proj[:, :, 1024:2048].reshape(b, s, 4, NSA_KV_HEADS, HEAD_DIM)
    keep = min(NSA_WINDOW, s)
    win = proj[:, s - keep:, 2048:2560].reshape(b, keep, 2, NSA_KV_HEADS, HEAD_DIM)
    return [o_cs, o_win], [kvf, win]


def _diff_lambda(lam_ref, lam_init):
    a = jnp.sum(lam_ref[0:1, :] * lam_ref[1:2, :], axis=-1, keepdims=True)
    b = jnp.sum(lam_ref[2:3, :] * lam_ref[3:4, :], axis=-1, keepdims=True)
    return jnp.exp(a) - jnp.exp(b) + lam_init


def _diff_kernel(qb_ref, kb_ref, q_ref, k_ref, v_ref, bias_ref, lam_ref, g_ref, o_ref, m_sc, l_sc, acc_sc,
                 *, lam_init):
    t = pl.program_id(1)
    qb = qb_ref[t]
    kb = kb_ref[t]

    @pl.when(kb == 0)
    def _():
        _flash_init(m_sc, l_sc, acc_sc)

    valid = _dist_tile(qb - kb) >= 0
    for hm in range(2 * DIFF_HEADS):
        hd = hm // 2
        q = (q_ref[0, :, hm * 64:(hm + 1) * 64] * SCALE).astype(BF16)
        k = k_ref[0, :, hm * 64:(hm + 1) * 64].astype(BF16)
        v = v_ref[0, :, hd * DIFF_VDIM:(hd + 1) * DIFF_VDIM].astype(BF16)
        s = jnp.where(valid, _dot_nt(q, k) + bias_ref[0, hm], NEG_INF)
        _flash_update(hm, s, v, m_sc, l_sc, acc_sc, hm * DIFF_VDIM, (hm + 1) * DIFF_VDIM)

    @pl.when(kb == qb)
    def _():
        lam = _diff_lambda(lam_ref, lam_init)
        for hd in range(DIFF_HEADS):
            o1 = acc_sc[:, (2 * hd) * DIFF_VDIM:(2 * hd + 1) * DIFF_VDIM] / l_sc[2 * hd]
            o2 = acc_sc[:, (2 * hd + 1) * DIFF_VDIM:(2 * hd + 2) * DIFF_VDIM] / l_sc[2 * hd + 1]
            o = o1 - lam * o2
            o = o * lax.rsqrt(jnp.mean(o * o, axis=-1, keepdims=True) + RMS_EPS) * g_ref[...] * (1.0 - lam_init)
            o_ref[0, :, hd * DIFF_VDIM:(hd + 1) * DIFF_VDIM] = o


def _diff_prompt(x, mod, w_in_bf16, lam, subln_g, lam_init, rel_bias):
    b, s, _ = x.shape
    proj = _mod_matmul(x, mod, w_in_bf16, tn=1024, tm=512)
    nqb = s // BLK
    qb, kb = _pair_list(nqb, nqb)
    bias = _bias_tiles(rel_bias, nqb, 1)
    o = pl.pallas_call(
        functools.partial(_diff_kernel, lam_init=lam_init),
        grid_spec=pltpu.PrefetchScalarGridSpec(
            num_scalar_prefetch=2,
            grid=(b, int(qb.shape[0])),
            in_specs=[pl.BlockSpec((1, BLK, 1024), lambda i, t, qb_, kb_: (i, qb_[t], 0)),
                      pl.BlockSpec((1, BLK, 1024), lambda i, t, qb_, kb_: (i, kb_[t], 1)),
                      pl.BlockSpec((1, BLK, 1024), lambda i, t, qb_, kb_: (i, kb_[t], 2)),
                      pl.BlockSpec((1, N_HEADS, BLK, BLK), lambda i, t, qb_, kb_: (qb_[t] - kb_[t], 0, 0, 0)),
                      pl.BlockSpec((4, HEAD_DIM), lambda i, t, qb_, kb_: (0, 0)),
                      pl.BlockSpec((1, DIFF_VDIM), lambda i, t, qb_, kb_: (0, 0))],
            out_specs=pl.BlockSpec((1, BLK, 1024), lambda i, t, qb_, kb_: (i, qb_[t], 0)),
            scratch_shapes=[pltpu.VMEM((N_HEADS, BLK, 1), F32),
                            pltpu.VMEM((N_HEADS, BLK, 1), F32),
                            pltpu.VMEM((BLK, N_HEADS * DIFF_VDIM), F32)]),
        out_shape=_sds((b, s, 1024)),
        compiler_params=_cparams(("arbitrary", "arbitrary")),
        name="diff_attention",
    )(qb, kb, proj, proj, proj, bias, lam, subln_g.reshape(1, DIFF_VDIM))
    rows = proj[:, :, 1024:3072].reshape(b, s, 2, DIFF_HEADS, DIFF_VDIM)
    return [o], [rows]


def _softmax_with_new(s, s0):
    m = jnp.maximum(jnp.max(s, axis=-1, keepdims=True), s0)
    p = jnp.exp(s - m)
    p0 = jnp.exp(s0 - m)
    return p, p0, jnp.sum(p, axis=-1, keepdims=True) + p0, m


def _dil_sample_kernel(proj_ref, b1_ref, b2_ref, b3_ref, bias_ref, bias0_ref, o_ref, *, nb):
    bufs = (b1_ref, b2_ref, b3_ref)
    row_id = lax.broadcasted_iota(jnp.int32, (N_HEADS, 1024), 0)
    lane = lax.broadcasted_iota(jnp.int32, (N_HEADS, 1024), 1)
    diag = lane // HEAD_DIM == row_id
    for i in range(nb):
        outs, lses = [], []
        for g in range(len(DIL_GROUPS)):
            base = g * 3072
            q = proj_ref[i, :, base:base + 1024]
            kn = proj_ref[i, :, base + 1024:base + 2048]
            vn = proj_ref[i, :, base + 2048:base + 3072]
            qbd = jnp.where(diag, q * SCALE, 0.0)
            k = bufs[g][i, :, 0:1024].astype(BF16)
            v = bufs[g][i, :, 1024:2048].astype(BF16)
            s = _dot_nt(qbd.astype(BF16), k) + bias_ref[g]
            s0 = jnp.sum(qbd * kn, axis=-1, keepdims=True) + bias0_ref[...]
            p, p0, l, m = _softmax_with_new(s, s0)
            outs.append((_dot(p.astype(BF16), v) + p0 * vn) / l)
            lses.append(m + jnp.log(l))
        mm = jnp.maximum(jnp.maximum(lses[0], lses[1]), lses[2])
        ws = [jnp.exp(l - mm) for l in lses]
        o = (ws[0] * outs[0] + ws[1] * outs[1] + ws[2] * outs[2]) / (ws[0] + ws[1] + ws[2])
        o_ref[i] = jnp.sum(jnp.where(diag, o, 0.0), axis=0, keepdims=True)


def _dilated_sample(proj_s, bufs, rel_bias):
    n = proj_s.shape[0]
    nb = 2
    views, biases = [], []
    for (win, dil), buf in zip(DIL_GROUPS, bufs):
        rows = win // dil
        views.append(buf.reshape(n, rows, dil * 2048))
        biases.append(_bias_rows(rel_bias, (rows - np.arange(rows)) * dil))
    bias = jnp.stack(biases, 0)
    bias0 = rel_bias[0].reshape(N_HEADS, 1)
    buf_spec = pl.BlockSpec((nb, 128, 2048), lambda i: (i, 0, 0))
    return pl.pallas_call(
        functools.partial(_dil_sample_kernel, nb=nb),
        grid=(n // nb,),
        in_specs=[pl.BlockSpec((nb, 1, 9216), lambda i: (i, 0, 0)), buf_spec, buf_spec, buf_spec,
                  pl.BlockSpec((3, N_HEADS, 128), lambda i: (0, 0, 0)),
                  pl.BlockSpec((N_HEADS, 1), lambda i: (0, 0))],
        out_specs=pl.BlockSpec((nb, 1, 1024), lambda i: (i, 0, 0)),
        out_shape=_sds((n, 1, 1024)),
        compiler_params=_cparams(("arbitrary",)),
        name="dilated_sample",
    )(proj_s.reshape(n, 1, 9216), *views, bias, bias0)


def _page_specs(width):
    return [pl.BlockSpec((1, PAGE_SIZE, width), lambda n, pt, j=j: (pt[n * N_PAGES + j], 0, 0))
            for j in range(N_PAGES)]


def _diff_sample_kernel(pt_ref, proj_ref, *refs, lam_init):
    pages = refs[:N_PAGES]
    bias_ref, bias0_ref, lam_ref, g_ref, o_ref = refs[N_PAGES:]
    nr = 2 * DIFF_HEADS
    row_id = lax.broadcasted_iota(jnp.int32, (nr, 1024), 0)
    lane = lax.broadcasted_iota(jnp.int32, (nr, 1024), 1)
    diag = lane // HEAD_DIM == 2 * (row_id % DIFF_HEADS) + row_id // DIFF_HEADS
    q = proj_ref[0, :, 0:1024]
    kn = proj_ref[0, :, 1024:2048]
    vn = proj_ref[0, :, 2048:3072]
    qbd = jnp.where(diag, q * SCALE, 0.0)
    qb16 = qbd.astype(BF16)
    s = jnp.concatenate([_dot_nt(qb16, pages[j][0, :, 0:1024].astype(BF16)) for j in range(N_PAGES)], axis=1)
    s = s + bias_ref[...]
    s0 = jnp.sum(qbd * kn, axis=-1, keepdims=True) + bias0_ref[...]
    p, p0, l, _ = _softmax_with_new(s, s0)
    lam = _diff_lambda(lam_ref, lam_init)
    p = p / l
    p0 = p0 / l
    attn = (p[0:DIFF_HEADS] - lam * p[DIFF_HEADS:]).astype(BF16)
    a0 = p0[0:DIFF_HEADS] - lam * p0[DIFF_HEADS:]
    o = a0 * vn
    for j in range(N_PAGES):
        o = o + _dot(attn[:, j * PAGE_SIZE:(j + 1) * PAGE_SIZE], pages[j][0, :, 1024:2048].astype(BF16))
    hrow = lax.broadcasted_iota(jnp.int32, (DIFF_HEADS, 1024), 0)
    hlane = lax.broadcasted_iota(jnp.int32, (DIFF_HEADS, 1024), 1)
    od = jnp.where(hlane // DIFF_VDIM == hrow, o, 0.0)
    ms = jnp.sum(od * od, axis=-1, keepdims=True) * (1.0 / DIFF_VDIM)
    on = od * lax.rsqrt(ms + RMS_EPS)
    o_ref[0] = jnp.sum(on, axis=0, keepdims=True) * g_ref[...] * (1.0 - lam_init)


def _diff_sample(proj_s, pool, page_table, lam, subln_g, lam_init, rel_bias):
    n = proj_s.shape[0]
    perm = np.asarray([2 * (r % DIFF_HEADS) + r // DIFF_HEADS for r in range(2 * DIFF_HEADS)])
    rb = rel_bias[:, perm]
    bias = _bias_rows(rb, PAST_LEN - np.arange(PAST_LEN))
    bias0 = rb[0].reshape(-1, 1)
    g_t = jnp.tile(subln_g, DIFF_HEADS).reshape(1, 1024)
    pool_v = pool.reshape(pool.shape[0], PAGE_SIZE, 2048)
    const2 = lambda shape: pl.BlockSpec(shape, lambda i, pt: (0, 0))
    return pl.pallas_call(
        functools.partial(_diff_sample_kernel, lam_init=lam_init),
        grid_spec=pltpu.PrefetchScalarGridSpec(
            num_scalar_prefetch=1,
            grid=(n,),
            in_specs=[pl.BlockSpec((1, 1, 3072), lambda i, pt: (i, 0, 0))] + _page_specs(2048) + [
                const2((2 * DIFF_HEADS, PAST_LEN)), const2((2 * DIFF_HEADS, 1)),
                const2((4, HEAD_DIM)), const2((1, 1024))],
            out_specs=pl.BlockSpec((1, 1, 1024), lambda i, pt: (i, 0, 0))),
        out_shape=_sds((n, 1, 1024)),
        compiler_params=_cparams(("arbitrary",)),
        name="diff_sample",
    )(page_table.reshape(-1), proj_s.reshape(n, 1, 3072), *([pool_v] * N_PAGES), bias, bias0, lam, g_t)


NSA_ROWS = 32


def _nsa_sample_kernel(pt_ref, proj_ref, *refs):
    pages = refs[:N_PAGES]
    win_ref, bd_ref, pe_ref, bsel_ref, bwin_ref, bias0_ref, o_ref, xs = refs[N_PAGES:]
    for j in range(N_PAGES):
        for c in range(4):
            xs[c, j * PAGE_SIZE:(j + 1) * PAGE_SIZE, :] = pages[j][0, :, c * LANES:(c + 1) * LANES]

    def load(i, start, kk):
        rows = pl.ds(start, N_CMP // 2, stride=2 * CMP_BLOCK)
        return jnp.concatenate([xs[2 * kk, rows, :], xs[2 * kk + 1, rows, :]], axis=1)

    kc, vc = _compress_rows(load, bd_ref, pe_ref, 1)

    row_id = lax.broadcasted_iota(jnp.int32, (NSA_ROWS, 256), 0)
    lane = lax.broadcasted_iota(jnp.int32, (NSA_ROWS, 256), 1)
    diag = lane // HEAD_DIM == row_id % 8
    qf = jnp.zeros((NSA_ROWS, 256), F32)
    for r in range(NSA_HPG):
        qf = jnp.where(row_id // 8 == r, proj_ref[0, :, r * 256:(r + 1) * 256], qf)
    qbd = jnp.where(diag, qf * SCALE, 0.0)
    qb16 = qbd.astype(BF16)
    kn_sel = proj_ref[0, :, 1536:1792]
    vn_sel = proj_ref[0, :, 1792:2048]
    kn_win = proj_ref[0, :, 2048:2304]
    vn_win = proj_ref[0, :, 2304:2560]
    gates = _sigmoid(proj_ref[0, :, 2560:2688])

    s = _dot_nt(qb16, kc.astype(BF16))
    e = jnp.exp(s - jnp.max(s, axis=-1, keepdims=True))
    p = e / jnp.sum(e, axis=-1, keepdims=True)
    o_cmp = _dot(p.astype(BF16), vc.astype(BF16))
    imp = p[0:8] + p[8:16] + p[16:24] + p[24:32]
    imp = imp[:, :N_SEL] + imp[:, N_SEL:]
    blk = lax.broadcasted_iota(jnp.int32, (8, N_SEL), 1)
    sel = _top_blocks(jnp.where(blk == 0, SEL_FORCE, imp), SEL_TOP - 1)
    selrows = jnp.concatenate([sel] * NSA_HPG, axis=0).astype(BF16)

    eb = lax.broadcasted_iota(jnp.int32, (N_SEL, PAGE_SIZE), 0)
    ek = lax.broadcasted_iota(jnp.int32, (N_SEL, PAGE_SIZE), 1)
    parts = []
    for j in range(N_PAGES):
        expand = jnp.where(eb == 2 * j + ek // SEL_BLOCK, 1.0, 0.0).astype(BF16)
        chosen = _dot(selrows, expand) > 0.5
        sj = _dot_nt(qb16, pages[j][0, :, 512:768].astype(BF16)) + bsel_ref[:, j * PAGE_SIZE:(j + 1) * PAGE_SIZE]
        parts.append(jnp.where(chosen, sj, NEG_INF))
    s = jnp.concatenate(parts, axis=1)
    s0 = jnp.sum(qbd * kn_sel, axis=-1, keepdims=True) + bias0_ref[...]
    p, p0, l, _ = _softmax_with_new(s, s0)
    p = p.astype(BF16)
    o_sel = p0 * vn_sel
    for j in range(N_PAGES):
        o_sel = o_sel + _dot(p[:, j * PAGE_SIZE:(j + 1) * PAGE_SIZE], pages[j][0, :, 768:1024].astype(BF16))
    o_sel = o_sel / l

    s = _dot_nt(qb16, win_ref[0, :, 0:256].astype(BF16)) + bwin_ref[...]
    s0 = jnp.sum(qbd * kn_win, axis=-1, keepdims=True) + bias0_ref[...]
    p, p0, l, _ = _softmax_with_new(s, s0)
    o_win = (_dot(p.astype(BF16), win_ref[0, :, 256:512].astype(BF16)) + p0 * vn_win) / l

    grow = lax.broadcasted_iota(jnp.int32, (NSA_ROWS, LANES), 0)
    glane = lax.broadcasted_iota(jnp.int32, (NSA_ROWS, LANES), 1)
    gcol = ((grow // 8) * 4 + grow % 8) * 3
    o = jnp.zeros((NSA_ROWS, 256), F32)
    for br, ob in enumerate((o_cmp, o_sel, o_win)):
        gsel = (glane == gcol + br) & (grow % 8 < NSA_KV_HEADS)
        gv = jnp.sum(jnp.where(gsel, gates, 0.0), axis=-1, keepdims=True)
        o = o + gv * ob
    o = jnp.where(diag, o, 0.0)
    for r in range(NSA_HPG):
        o_ref[0, :, r * 256:(r + 1) * 256] = jnp.sum(o[r * 8:(r + 1) * 8], axis=0, keepdims=True)


def _nsa_sample(proj_s, pool, page_table, win_buf, bd, pe_t, rb_p):
    n = proj_s.shape[0]

    def rows32(bias):
        j = bias.shape[1]
        b4 = bias.reshape(NSA_HPG, NSA_KV_HEADS, j)
        return jnp.concatenate([b4, jnp.zeros_like(b4)], axis=1).reshape(NSA_ROWS, j)

    bsel = rows32(_bias_rows(rb_p, PAST_LEN - np.arange(PAST_LEN)))
    bwin = rows32(_bias_rows(rb_p, NSA_WINDOW - np.arange(NSA_WINDOW)))
    bias0 = rows32(rb_p[0].reshape(N_HEADS, 1))
    pool_v = pool.reshape(pool.shape[0], PAGE_SIZE, 1024)
    win_v = win_buf.reshape(n, NSA_WINDOW, 512)
    const = lambda shape: pl.BlockSpec(shape, lambda i, pt: (0,) * len(shape))
    return pl.pallas_call(
        _nsa_sample_kernel,
        grid_spec=pltpu.PrefetchScalarGridSpec(
            num_scalar_prefetch=1,
            grid=(n,),
            in_specs=[pl.BlockSpec((1, 1, W_IN_B_PAD), lambda i, pt: (i, 0, 0))] + _page_specs(1024) + [
                pl.BlockSpec((1, NSA_WINDOW, 512), lambda i, pt: (i, 0, 0)),
                const((2, CMP_BLOCK, 256, 256)), const((2, CMP_BLOCK, 1, 256)),
                const((NSA_ROWS, PAST_LEN)), const((NSA_ROWS, NSA_WINDOW)), const((NSA_ROWS, 1))],
            out_specs=pl.BlockSpec((1, 1, 1024), lambda i, pt: (i, 0, 0)),
            scratch_shapes=[pltpu.VMEM((4, PAST_LEN, LANES), F32)]),
        out_shape=_sds((n, 1, 1024)),
        compiler_params=_cparams(("arbitrary",)),
        name="nsa_sample",
    )(page_table.reshape(-1), proj_s.reshape(n, 1, W_IN_B_PAD), *([pool_v] * N_PAGES), win_v, bd, pe_t,
      bsel, bwin, bias0)


TM_PROMPT = 256


def kernel(x_prompt, x_sample, c_prompt, c_sample, page_table, state_l0_win1, state_l0_win2, state_l0_win3, cache_l1_kv, state_l1_win, cache_l2_kv, state_l3_win1, state_l3_win2, state_l3_win3, rel_bias, l0_w_in, l0_w_o, l0_ada_w, l0_ada_b, l0_ln_g, l0_ln_b, l0_w_gu, l0_w_down, l1_w_in, l1_w_o, l1_ada_w, l1_ada_b, l1_ln_g, l1_ln_b, l1_w_gu, l1_w_down, l1_cmp_pe, l1_cmp_w, l2_w_in, l2_w_o, l2_ada_w, l2_ada_b, l2_ln_g, l2_ln_b, l2_w_gu, l2_w_down, l2_lam, l2_subln_g, l3_w_in, l3_w_o, l3_ada_w, l3_ada_b, l3_ln_g, l3_ln_b, l3_w_gu, l3_w_down):
    layers = (
        (l0_w_in, l0_w_o, l0_ada_w, l0_ada_b, l0_ln_g, l0_ln_b, l0_w_gu, l0_w_down),
        (l1_w_in, l1_w_o, l1_ada_w, l1_ada_b, l1_ln_g, l1_ln_b, l1_w_gu, l1_w_down),
        (l2_w_in, l2_w_o, l2_ada_w, l2_ada_b, l2_ln_g, l2_ln_b, l2_w_gu, l2_w_down),
        (l3_w_in, l3_w_o, l3_ada_w, l3_ada_b, l3_ln_g, l3_ln_b, l3_w_gu, l3_w_down),
    )
    dil_caches = {0: (state_l0_win1, state_l0_win2, state_l0_win3), 3: (state_l3_win1, state_l3_win2, state_l3_win3)}
    n = x_sample.shape[0]
    c_all = jnp.concatenate([c_prompt, c_sample], axis=0)
    xp = x_prompt
    xs = x_sample.reshape(1, n, D_MODEL)
    new_p, new_s = [], []
    for i in range(DEPTH):
        w_in, w_o, ada_w, ada_b, ln_g, ln_b, w_gu, w_down = layers[i]
        mod = _ada(c_all, ada_w, ada_b)
        mp = mod[:BATCH].reshape(BATCH, 1, 6 * D_MODEL)
        ms = mod[BATCH:].reshape(1, n, 6 * D_MODEL)
        kind = i % 3
        if kind == 0:
            w_in_b = w_in.astype(BF16)
            w_o_b = w_o.astype(BF16)
            ins_p, st_p = _dilated_prompt(xp, mp, w_in_b, rel_bias)
            proj_s = _mod_matmul(xs, ms, w_in_b, tn=1024, tm=n)[0]
            ins_s = [_dilated_sample(proj_s, dil_caches[i], rel_bias).reshape(1, n, 1024)]
            st_s = [proj_s[:, g * 3072 + 1024:(g + 1) * 3072].reshape(n, 1, 2, N_HEADS, HEAD_DIM)
                    for g in range(len(DIL_GROUPS))]
            combine = True
        elif kind == 1:
            w_in_b, w_o_b, bd, pe_t, rb_p = _nsa_weights(w_in, w_o, l1_cmp_pe, l1_cmp_w, rel_bias)
            ins_p, st_p = _nsa_prompt(xp, mp, w_in_b, bd, pe_t, rb_p)
            proj_s = _mod_matmul(xs, ms, w_in_b, tn=896, tm=n)[0]
            ins_s = [_nsa_sample(proj_s, cache_l1_kv, page_table, state_l1_win, bd, pe_t, rb_p).reshape(1, n, 1024)]
            st_s = [proj_s[:, 1024:2048].reshape(n, 1, 4, NSA_KV_HEADS, HEAD_DIM),
                    proj_s[:, 2048:2560].reshape(n, 1, 2, NSA_KV_HEADS, HEAD_DIM)]
            combine = False
        else:
            lam_init = 0.8 - 0.6 * math.exp(-0.3 * i)
            w_in_b = w_in.astype(BF16)
            w_o_b = w_o.astype(BF16)
            ins_p, st_p = _diff_prompt(xp, mp, w_in_b, l2_lam, l2_subln_g, lam_init, rel_bias)
            proj_s = _mod_matmul(xs, ms, w_in_b, tn=1024, tm=n)[0]
            ins_s = [_diff_sample(proj_s, cache_l2_kv, page_table, l2_lam, l2_subln_g, lam_init,
                                  rel_bias).reshape(1, n, 1024)]
            st_s = [proj_s[:, 1024:3072].reshape(n, 1, 2, DIFF_HEADS, DIFF_VDIM)]
            combine = False
        new_p += st_p
        new_s += st_s
        xp = _oproj(ins_p, xp, mp, w_o_b, ln_g, ln_b, tm=TM_PROMPT, combine=combine)
        xs = _oproj(ins_s, xs, ms, w_o_b, ln_g, ln_b, tm=n)
        w_gu_b = w_gu.astype(BF16)
        w_down_b = w_down.astype(BF16)
        xp = _swiglu(xp, mp, w_gu_b, w_down_b, ln_g, ln_b, tm=TM_PROMPT)
        xs = _swiglu(xs, ms, w_gu_b, w_down_b, ln_g, ln_b, tm=n)
    return (xp, xs.reshape(n, 1, D_MODEL), *new_p, *new_s)
```
